```python
import jax, jax.numpy as jnp
from jax import lax
import numpy as np

D_MODEL = 1024
BATCH = 4
SEQ = 8192
DEPTH = 2

CHUNK = 64
EXPAND = 2
E_A = EXPAND * D_MODEL // 2
E_B = EXPAND * D_MODEL // 2
A_GROUPS = 8
A_BLOCK = 128
B_GROUPS = 8
CONV_WIDTH = 31
PLE_DIM = 256
NORM_EPS = 1e-6
LN_EPS = 1e-5
N_IN = 3 * E_A + 3 * E_B + 2 * D_MODEL
SPLIT_IDX = [E_A, 2 * E_A, 3 * E_A, 3 * E_A + E_B, 3 * E_A + 2 * E_B,
             3 * E_A + 3 * E_B, 3 * E_A + 3 * E_B + D_MODEL]

kernel_name = "hybrid_gmlp_conformer_gated_trunk"


def _rmsnorm(x, g):
    xf = x.astype(jnp.float32)
    y = xf * lax.rsqrt(jnp.mean(xf * xf, axis=-1, keepdims=True) + NORM_EPS)
    return (y * g.astype(jnp.float32)).astype(x.dtype)


def _normalize(x):
    xf = x.astype(jnp.float32)
    mu = jnp.mean(xf, axis=-1, keepdims=True)
    var = jnp.mean(jnp.square(xf - mu), axis=-1, keepdims=True)
    return (xf - mu) * lax.rsqrt(var + LN_EPS)


def _chunk_causal_mask():
    c = jnp.arange(A_BLOCK) // CHUNK
    return c[:, None] >= c[None, :]


def _spatial_gating_branch(u, v, z, ln_g, ln_b, ws, bs):
    b_, s_, _ = v.shape
    v = (_normalize(v) * ln_g.astype(jnp.float32) + ln_b.astype(jnp.float32)).astype(u.dtype)
    ws = jnp.where(_chunk_causal_mask()[None], ws, jnp.zeros_like(ws))
    vb = v.reshape(b_, s_ // A_BLOCK, A_BLOCK, A_GROUPS, E_A // A_GROUPS)
    mixed = jnp.einsum('gij,bnjgc->bnigc', ws, vb) + bs.T[:, :, None]
    mixed = mixed.reshape(b_, s_, E_A)
    return u * mixed * jax.nn.silu(z)


def _conformer_conv_branch(a, a_gate, z, conv_w, conv_b, gn_g, gn_b):
    h = a * jax.nn.sigmoid(a_gate)
    h = lax.conv_general_dilated(
        h, conv_w[:, None, :].astype(h.dtype), window_strides=(1,),
        padding=[(CONV_WIDTH - 1, 0)], dimension_numbers=('NWC', 'WIO', 'NWC'),
        feature_group_count=E_B) + conv_b
    b_, s_, _ = h.shape
    hn = _normalize(h.reshape(b_, s_, B_GROUPS, E_B // B_GROUPS)).reshape(b_, s_, E_B)
    hn = (hn * gn_g.astype(jnp.float32) + gn_b.astype(jnp.float32)).astype(a.dtype)
    return jax.nn.silu(hn) * jax.nn.silu(z)


def setup_inputs(seed: int = 0) -> dict:
    key = jax.random.key(seed)
    ks = jax.random.split(key, 20)
    f32 = jnp.float32
    nrm = lambda k, shape, scale: jax.random.normal(k, shape, f32) * scale
    return {
        "x": nrm(ks[0], (BATCH, SEQ, D_MODEL), 1.0),
        "p": nrm(ks[1], (DEPTH, BATCH, SEQ, PLE_DIM), 1.0),
        "norm_g": 1.0 + nrm(ks[2], (DEPTH, D_MODEL), 0.05),
        "w_in": nrm(ks[3], (DEPTH, D_MODEL, N_IN), D_MODEL ** -0.5),
        "b_in": nrm(ks[4], (DEPTH, N_IN), 0.02),
        "a_ln_g": 1.0 + nrm(ks[5], (DEPTH, E_A), 0.05),
        "a_ln_b": nrm(ks[6], (DEPTH, E_A), 0.02),
        "a_ws": nrm(ks[7], (DEPTH, A_GROUPS, A_BLOCK, A_BLOCK), A_BLOCK ** -0.5),
        "a_bs": 1.0 + nrm(ks[8], (DEPTH, A_GROUPS, A_BLOCK), 0.1),
        "b_conv_w": nrm(ks[9], (DEPTH, CONV_WIDTH, E_B), CONV_WIDTH ** -0.5),
        "b_conv_b": nrm(ks[10], (DEPTH, E_B), 0.02),
        "b_gn_g": 1.0 + nrm(ks[11], (DEPTH, E_B), 0.05),
        "b_gn_b": nrm(ks[12], (DEPTH, E_B), 0.02),
        "w_pa": nrm(ks[13], (DEPTH, E_A, D_MODEL), E_A ** -0.5),
        "w_pb": nrm(ks[14], (DEPTH, E_B, D_MODEL), E_B ** -0.5),
        "w_out": nrm(ks[15], (DEPTH, D_MODEL, D_MODEL), D_MODEL ** -0.5),
        "ple_norm_g": 1.0 + nrm(ks[16], (DEPTH, D_MODEL), 0.05),
        "w_ple_gate": nrm(ks[17], (DEPTH, D_MODEL, D_MODEL), D_MODEL ** -0.5),
        "w_ple": nrm(ks[18], (DEPTH, PLE_DIM, D_MODEL), PLE_DIM ** -0.5),
        "final_g": 1.0 + nrm(ks[19], (D_MODEL,), 0.05),
    }


def reference(x, p, norm_g, w_in, b_in, a_ln_g, a_ln_b, a_ws, a_bs, b_conv_w, b_conv_b,
              b_gn_g, b_gn_b, w_pa, w_pb, w_out, ple_norm_g, w_ple_gate, w_ple, final_g):
    for i in range(DEPTH):
        h = _rmsnorm(x, norm_g[i])
        proj = jnp.einsum('bsd,dn->bsn', h, w_in[i]) + b_in[i]
        u, v, za, ab, ab_gate, zb, g_a, g_b = jnp.split(proj, SPLIT_IDX, axis=-1)
        y_a = _spatial_gating_branch(jax.nn.gelu(u), jax.nn.gelu(v), za,
                                     a_ln_g[i], a_ln_b[i], a_ws[i], a_bs[i])
        y_b = _conformer_conv_branch(ab, ab_gate, zb, b_conv_w[i], b_conv_b[i],
                                     b_gn_g[i], b_gn_b[i])
        merged = (jax.nn.sigmoid(g_a) * jnp.einsum('bse,ed->bsd', y_a, w_pa[i])
                  + jax.nn.sigmoid(g_b) * jnp.einsum('bse,ed->bsd', y_b, w_pb[i]))
        x = x + jnp.einsum('bsd,de->bse', merged, w_out[i])
        ple_gate = jax.nn.sigmoid(jnp.einsum('bsd,de->bse', _rmsnorm(x, ple_norm_g[i]), w_ple_gate[i]))
        x = x + ple_gate * jnp.einsum('bsk,kd->bsd', p[i], w_ple[i])
    return _rmsnorm(x, final_g)
```

```python
import functools

import jax
import jax.numpy as jnp
from jax import lax
from jax.experimental import pallas as pl
from jax.experimental.pallas import tpu as pltpu

D_MODEL = 1024
E_A = 1024
E_B = 1024
A_GROUPS = 8
A_BLOCK = 128
CHUNK = 64
B_GROUPS = 8
CONV_WIDTH = 31
PLE_DIM = 256
NORM_EPS = 1e-6
LN_EPS = 1e-5
SEG = 1024

LANES = 128
SUBLANES = 8
BF16_ROWS = 16
HALO = 32
CONV_ROWS = 32
TM = 256
VMEM_LIMIT_BYTES = 56 * 1024 * 1024

_SEG_U, _SEG_V, _SEG_ZA, _SEG_AB, _SEG_GATE, _SEG_ZB, _SEG_GA, _SEG_GB = (
    i * SEG for i in range(8))

_GELU_C0 = 0.7978845608028654
_GELU_C1 = 0.044715


def _sigmoid(x):
    return 0.5 * jnp.tanh(0.5 * x) + 0.5


def _silu(x):
    hx = 0.5 * x
    return hx * jnp.tanh(hx) + hx


def _gelu_tanh(x):
    inner = _GELU_C0 * (x + _GELU_C1 * (x * x * x))
    hx = 0.5 * x
    return hx * jnp.tanh(inner) + hx


def _row_loop(n_rows, step, body):
    for i in range(n_rows // step):
        body(i * step)


def _layer_kernel(x_ref, p_ref, ng_ref, win_ref, bin_ref, alg_ref, alb_ref, ws_ref, bs_ref,
                  cw_ref, cb_ref, gng_ref, gnb_ref, wpa_ref, wpb_ref, wout_ref, png_ref,
                  wpg_ref, wple_ref, fg_ref,
                  o_ref,
                  hb, f0, f1, f2, vb, yb, hbuf, wsm, *, tm, final):
    f32 = jnp.float32
    bf16 = jnp.bfloat16
    t = pl.program_id(1)
    R = BF16_ROWS

    @pl.when(t == 0)
    def _():
        hbuf[:, 0:HALO, :] = jnp.zeros((B_GROUPS, HALO, LANES), f32)
        ci = lax.broadcasted_iota(jnp.int32, (A_BLOCK, A_BLOCK), 0) // CHUNK
        cj = lax.broadcasted_iota(jnp.int32, (A_BLOCK, A_BLOCK), 1) // CHUNK
        keep = ci >= cj
        for g in range(A_GROUPS):
            wsm[g] = jnp.where(keep, ws_ref[g], 0.0).astype(bf16)

    def bias(off):
        return bin_ref[:, off:off + SEG]

    def seg_dot(dst, lhs, off):
        dst[...] = jnp.dot(lhs[...], win_ref[:, off:off + SEG], preferred_element_type=f32)

    def norm_in(r):
        xs = x_ref[0, pl.ds(r, R), :]
        ms = jnp.mean(xs * xs, axis=-1, keepdims=True)
        hb[pl.ds(r, R), :] = (xs * lax.rsqrt(ms + NORM_EPS) * ng_ref[...]).astype(bf16)
    _row_loop(tm, R, norm_in)

    seg_dot(f0, hb, _SEG_U)
    seg_dot(f1, hb, _SEG_V)

    def uv_epilogue(r):
        rows = pl.ds(r, R)
        f0[rows, :] = _gelu_tanh(f0[rows, :] + bias(_SEG_U))
        v = _gelu_tanh(f1[rows, :] + bias(_SEG_V))
        mu = jnp.mean(v, axis=-1, keepdims=True)
        d = v - mu
        var = jnp.mean(d * d, axis=-1, keepdims=True)
        vb[rows, :] = (d * lax.rsqrt(var + LN_EPS) * alg_ref[...] + alb_ref[...]).astype(bf16)
    _row_loop(tm, R, uv_epilogue)

    seg_dot(f1, hb, _SEG_ZA)

    def mix_block(r):
        rows = pl.ds(r, A_BLOCK)
        for g in range(A_GROUPS):
            cs = slice(g * LANES, (g + 1) * LANES)
            m = jnp.dot(wsm[g], vb[rows, cs], preferred_element_type=f32) + bs_ref[:, cs]
            z = f1[rows, cs] + bin_ref[:, _SEG_ZA + g * LANES:_SEG_ZA + (g + 1) * LANES]
            yb[rows, cs] = (f0[rows, cs] * m * _silu(z)).astype(bf16)
    _row_loop(tm, A_BLOCK, mix_block)

    f2[...] = jnp.dot(yb[...], wpa_ref[...], preferred_element_type=f32)

    seg_dot(f0, hb, _SEG_AB)
    seg_dot(f1, hb, _SEG_GATE)

    def glu(r):
        rows = pl.ds(r, R)
        for c in range(B_GROUPS):
            cs = slice(c * LANES, (c + 1) * LANES)
            a = f0[rows, cs] + bin_ref[:, _SEG_AB + c * LANES:_SEG_AB + (c + 1) * LANES]
            gt = f1[rows, cs] + bin_ref[:, _SEG_GATE + c * LANES:_SEG_GATE + (c + 1) * LANES]
            hbuf[c, pl.ds(HALO + r, R), :] = a * _sigmoid(gt)
    _row_loop(tm, R, glu)

    seg_dot(f0, hb, _SEG_ZB)

    base = HALO - (CONV_WIDTH - 1)
    for c in range(B_GROUPS):
        cs = slice(c * LANES, (c + 1) * LANES)
        wk = [jnp.broadcast_to(cw_ref[k:k + 1, cs], (SUBLANES, LANES)) for k in range(CONV_WIDTH)]
        cb = jnp.broadcast_to(cb_ref[:, cs], (SUBLANES, LANES))
        gg = gng_ref[:, cs]
        gb = gnb_ref[:, cs]
        bz = bin_ref[:, _SEG_ZB + c * LANES:_SEG_ZB + (c + 1) * LANES]

        def conv_rows(r, c=c, cs=cs, wk=wk, cb=cb, gg=gg, gb=gb, bz=bz):
            accs = [cb] * (CONV_ROWS // SUBLANES)
            for k in range(CONV_WIDTH):
                for h in range(CONV_ROWS // SUBLANES):
                    win = hbuf[pl.ds(c, 1, stride=2),
                               pl.ds(r + (base + h * SUBLANES + k), SUBLANES), :]
                    accs[h] = accs[h] + wk[k] * win.reshape(SUBLANES, LANES)
            y = jnp.concatenate(accs, axis=0)
            mu = jnp.mean(y, axis=-1, keepdims=True)
            d = y - mu
            var = jnp.mean(d * d, axis=-1, keepdims=True)
            hn = d * lax.rsqrt(var + LN_EPS) * gg + gb
            z = f0[pl.ds(r, CONV_ROWS), cs] + bz
            yb[pl.ds(r, CONV_ROWS), cs] = (_silu(hn) * _silu(z)).astype(bf16)
        _row_loop(tm, CONV_ROWS, conv_rows)

    hbuf[:, 0:HALO, :] = hbuf[:, tm:tm + HALO, :]

    f0[...] = jnp.dot(yb[...], wpb_ref[...], preferred_element_type=f32)

    seg_dot(f1, hb, _SEG_GA)

    def merge_a(r):
        rows = pl.ds(r, R)
        f2[rows, :] = _sigmoid(f1[rows, :] + bias(_SEG_GA)) * f2[rows, :]
    _row_loop(tm, R, merge_a)

    seg_dot(f1, hb, _SEG_GB)

    def merge_b(r):
        rows = pl.ds(r, R)
        yb[rows, :] = (f2[rows, :] + _sigmoid(f1[rows, :] + bias(_SEG_GB)) * f0[rows, :]).astype(bf16)
    _row_loop(tm, R, merge_b)

    f0[...] = jnp.dot(yb[...], wout_ref[...], preferred_element_type=f32)

    def resid(r):
        rows = pl.ds(r, R)
        x1 = x_ref[0, rows, :] + f0[rows, :]
        f0[rows, :] = x1
        ms = jnp.mean(x1 * x1, axis=-1, keepdims=True)
        hb[rows, :] = (x1 * lax.rsqrt(ms + NORM_EPS) * png_ref[...]).astype(bf16)
    _row_loop(tm, R, resid)

    f1[...] = jnp.dot(hb[...], wpg_ref[...], preferred_element_type=f32)
    f2[...] = jnp.dot(p_ref[0, 0].astype(bf16), wple_ref[...], preferred_element_type=f32)

    def ple(r):
        rows = pl.ds(r, R)
        x2 = f0[rows, :] + _sigmoid(f1[rows, :]) * f2[rows, :]
        if final:
            ms = jnp.mean(x2 * x2, axis=-1, keepdims=True)
            x2 = x2 * lax.rsqrt(ms + NORM_EPS) * fg_ref[...]
        o_ref[0, rows, :] = x2
    _row_loop(tm, R, ple)


def _resident(shape):
    nd = len(shape)
    return pl.BlockSpec(shape, lambda b, t: (0,) * nd, pipeline_mode=pl.Buffered(1))


def _layer_call(layer, final, x, p, norm_g, w_in, b_in, a_ln_g, a_ln_b, a_ws, bs_full,
                conv_w, conv_b, gn_g, gn_b, w_pa, w_pb, w_out, ple_norm_g, w_ple_gate, w_ple,
                final_g):
    batch, seq, d = x.shape
    tm = TM
    assert seq % tm == 0 and tm % A_BLOCK == 0 and d == D_MODEL
    f32 = jnp.float32
    bf16 = jnp.bfloat16
    operands = (x, p, norm_g, w_in, b_in, a_ln_g, a_ln_b, a_ws, bs_full, conv_w, conv_b, gn_g,
                gn_b, w_pa, w_pb, w_out, ple_norm_g, w_ple_gate, w_ple, final_g)
    in_specs = [
        pl.BlockSpec((1, tm, d), lambda b, t: (b, t, 0)),
        pl.BlockSpec((1, 1, tm, PLE_DIM), lambda b, t: (layer, b, t, 0)),
    ] + [_resident(a.shape) for a in operands[2:]]
    kern = functools.partial(_layer_kernel, tm=tm, final=final)
    return pl.pallas_call(
        kern,
        grid=(batch, seq // tm),
        in_specs=in_specs,
        out_specs=pl.BlockSpec((1, tm, d), lambda b, t: (b, t, 0)),
        out_shape=jax.ShapeDtypeStruct(x.shape, f32),
        scratch_shapes=[
            pltpu.VMEM((tm, d), bf16),
            pltpu.VMEM((tm, d), f32),
            pltpu.VMEM((tm, d), f32),
            pltpu.VMEM((tm, d), f32),
            pltpu.VMEM((tm, E_A), bf16),
            pltpu.VMEM((tm, d), bf16),
            pltpu.VMEM((B_GROUPS, HALO + tm, LANES), f32),
            pltpu.VMEM((A_GROUPS, A_BLOCK, A_BLOCK), bf16),
        ],
        compiler_params=pltpu.CompilerParams(
            dimension_semantics=("arbitrary", "arbitrary"),
            vmem_limit_bytes=VMEM_LIMIT_BYTES),
        name=f"trunk_layer{layer}",
    )(*operands)


def kernel(x, p, norm_g, w_in, b_in, a_ln_g, a_ln_b, a_ws, a_bs, b_conv_w, b_conv_b, b_gn_g,
           b_gn_b, w_pa, w_pb, w_out, ple_norm_g, w_ple_gate, w_ple, final_g):
    depth = w_in.shape[0]
    bf16 = jnp.bfloat16
    row = lambda a: a.reshape(1, -1)
    for i in range(depth):
        bs_full = jnp.repeat(a_bs[i].T, E_A // A_GROUPS, axis=1)
        x = _layer_call(
            i, i == depth - 1, x, p, row(norm_g[i]), w_in[i].astype(bf16), row(b_in[i]),
            row(a_ln_g[i]), row(a_ln_b[i]), a_ws[i], bs_full, b_conv_w[i], row(b_conv_b[i]),
            row(b_gn_g[i]), row(b_gn_b[i]), w_pa[i].astype(bf16), w_pb[i].astype(bf16),
            w_out[i].astype(bf16), row(ple_norm_g[i]), w_ple_gate[i].astype(bf16),
            w_ple[i].astype(bf16), row(final_g))
    return x
```

```python
import functools

import jax
import jax.numpy as jnp
from jax import lax
from jax.experimental import pallas as pl
from jax.experimental.pallas import tpu as pltpu

D_MODEL = 1024
E_A = 1024
E_B = 1024
A_GROUPS = 8
A_BLOCK = 128
CHUNK = 64
B_GROUPS = 8
CONV_WIDTH = 31
PLE_DIM = 256
NORM_EPS = 1e-6
LN_EPS = 1e-5
SEG = 1024
N_SEG = 8

LANES = 128
SUBLANES = 8
BF16_ROWS = 16
HALO = 32
CONV_ROWS = 32
TM = 256
VMEM_LIMIT_BYTES = 58 * 1024 * 1024
N_WEIGHTS = 6

_U, _V, _ZA, _AB, _GATE, _ZB, _GA, _GB = range(N_SEG)

_GELU_C0 = 0.7978845608028654
_GELU_C1 = 0.044715


def _sigmoid(x):
    return 0.5 * jnp.tanh(0.5 * x) + 0.5


def _silu(x):
    hx = 0.5 * x
    return hx * jnp.tanh(hx) + hx


def _gelu_tanh(x):
    inner = x * (_GELU_C0 + (_GELU_C0 * _GELU_C1) * (x * x))
    hx = 0.5 * x
    return hx * jnp.tanh(inner) + hx


def _row_loop(n_rows, step, body):
    for i in range(n_rows // step):
        body(i * step)


def _weight_copies(hbm_refs, vmem_refs, sem):
    return [pltpu.make_async_copy(h, v, sem.at[i])
            for i, (h, v) in enumerate(zip(hbm_refs, vmem_refs))]


def _layer_kernel(x_ref, p_ref, ng_ref, bin_ref, alg_ref, alb_ref, ws_ref, bs_ref,
                  cw_ref, cb_ref, gng_ref, gnb_ref, png_ref, fg_ref,
                  win_hbm, wpa_hbm, wpb_hbm, wout_hbm, wpg_hbm, wple_hbm,
                  o_ref,
                  win, wpa, wpb, wout, wpg, wple, wsem,
                  hb, seg, pa, pb, po, x1b, pg, pe, vb, ya, yb, mg, rb, hbuf, wsm, *, tm, final):
    f32 = jnp.float32
    bf16 = jnp.bfloat16
    b = pl.program_id(0)
    t = pl.program_id(1)
    R = BF16_ROWS

    @pl.when((b == 0) & (t == 0))
    def _():
        copies = _weight_copies((win_hbm, wpa_hbm, wpb_hbm, wout_hbm, wpg_hbm, wple_hbm),
                                (win, wpa, wpb, wout, wpg, wple), wsem)
        for cp in copies:
            cp.start()
        for cp in copies:
            cp.wait()

    @pl.when(t == 0)
    def _():
        hbuf[:, 0:HALO, :] = jnp.zeros((B_GROUPS, HALO, LANES), f32)
        ci = lax.broadcasted_iota(jnp.int32, (A_BLOCK, A_BLOCK), 0) // CHUNK
        cj = lax.broadcasted_iota(jnp.int32, (A_BLOCK, A_BLOCK), 1) // CHUNK
        keep = ci >= cj
        for g in range(A_GROUPS):
            wsm[g] = jnp.where(keep, ws_ref[g], 0.0).astype(bf16)

    def bias(s, c=None):
        if c is None:
            return bin_ref[:, s * SEG:(s + 1) * SEG]
        return bin_ref[:, s * SEG + c * LANES:s * SEG + (c + 1) * LANES]

    def seg_dot(s):
        seg[s] = jnp.dot(hb[...], win[:, s * SEG:(s + 1) * SEG], preferred_element_type=f32)

    def norm_in(r):
        xs = x_ref[0, pl.ds(r, R), :]
        ms = jnp.mean(xs * xs, axis=-1, keepdims=True)
        hb[pl.ds(r, R), :] = (xs * lax.rsqrt(ms + NORM_EPS) * ng_ref[...]).astype(bf16)
    _row_loop(tm, R, norm_in)

    seg_dot(_U)
    seg_dot(_V)

    def uv_epilogue(r):
        rows = pl.ds(r, R)
        seg[_U, rows, :] = _gelu_tanh(seg[_U, rows, :] + bias(_U))
        v = _gelu_tanh(seg[_V, rows, :] + bias(_V))
        mu = jnp.mean(v, axis=-1, keepdims=True)
        d = v - mu
        var = jnp.mean(d * d, axis=-1, keepdims=True)
        vb[rows, :] = (d * lax.rsqrt(var + LN_EPS) * alg_ref[...] + alb_ref[...]).astype(bf16)
    _row_loop(tm, R, uv_epilogue)

    seg_dot(_ZA)

    def mix_block(r):
        rows = pl.ds(r, A_BLOCK)
        for g in range(A_GROUPS):
            cs = slice(g * LANES, (g + 1) * LANES)
            m = jnp.dot(wsm[g], vb[rows, cs], preferred_element_type=f32) + bs_ref[:, cs]
            z = seg[_ZA, rows, cs] + bias(_ZA, g)
            ya[rows, cs] = (seg[_U, rows, cs] * m * _silu(z)).astype(bf16)
    _row_loop(tm, A_BLOCK, mix_block)

    pa[...] = jnp.dot(ya[...], wpa[...], preferred_element_type=f32)

    seg_dot(_AB)
    seg_dot(_GATE)

    def glu(r):
        rows = pl.ds(r, R)
        for c in range(B_GROUPS):
            cs = slice(c * LANES, (c + 1) * LANES)
            a = seg[_AB, rows, cs] + bias(_AB, c)
            gt = seg[_GATE, rows, cs] + bias(_GATE, c)
            hbuf[c, pl.ds(HALO + r, R), :] = a * _sigmoid(gt)
    _row_loop(tm, R, glu)

    seg_dot(_ZB)

    base = HALO - (CONV_WIDTH - 1)
    for c in range(B_GROUPS):
        cs = slice(c * LANES, (c + 1) * LANES)
        wk = [jnp.broadcast_to(cw_ref[k:k + 1, cs], (SUBLANES, LANES)) for k in range(CONV_WIDTH)]
        cb = jnp.broadcast_to(cb_ref[:, cs], (SUBLANES, LANES))
        gg = gng_ref[:, cs]
        gb = gnb_ref[:, cs]
        bz = bias(_ZB, c)

        def conv_rows(r, c=c, cs=cs, wk=wk, cb=cb, gg=gg, gb=gb, bz=bz):
            accs = [cb] * (CONV_ROWS // SUBLANES)
            for k in range(CONV_WIDTH):
                for h in range(CONV_ROWS // SUBLANES):
                    win_k = hbuf[pl.ds(c, 1, stride=2),
                                 pl.ds(r + (base + h * SUBLANES + k), SUBLANES), :]
                    accs[h] = accs[h] + wk[k] * win_k.reshape(SUBLANES, LANES)
            y = jnp.concatenate(accs, axis=0)
            mu = jnp.mean(y, axis=-1, keepdims=True)
            d = y - mu
            var = jnp.mean(d * d, axis=-1, keepdims=True)
            hn = d * lax.rsqrt(var + LN_EPS) * gg + gb
            z = seg[_ZB, pl.ds(r, CONV_ROWS), cs] + bz
            yb[pl.ds(r, CONV_ROWS), cs] = (_silu(hn) * _silu(z)).astype(bf16)
        _row_loop(tm, CONV_ROWS, conv_rows)

    hbuf[:, 0:HALO, :] = hbuf[:, tm:tm + HALO, :]

    pb[...] = jnp.dot(yb[...], wpb[...], preferred_element_type=f32)

    seg_dot(_GA)
    seg_dot(_GB)

    def merge(r):
        rows = pl.ds(r, R)
        ma = _sigmoid(seg[_GA, rows, :] + bias(_GA)) * pa[rows, :]
        mb = _sigmoid(seg[_GB, rows, :] + bias(_GB)) * pb[rows, :]
        mg[rows, :] = (ma + mb).astype(bf16)
    _row_loop(tm, R, merge)

    po[...] = jnp.dot(mg[...], wout[...], preferred_element_type=f32)
    pe[...] = jnp.dot(p_ref[0, 0].astype(bf16), wple[...], preferred_element_type=f32)

    def resid(r):
        rows = pl.ds(r, R)
        x1 = x_ref[0, rows, :] + po[rows, :]
        x1b[rows, :] = x1
        ms = jnp.mean(x1 * x1, axis=-1, keepdims=True)
        rb[rows, :] = (x1 * lax.rsqrt(ms + NORM_EPS) * png_ref[...]).astype(bf16)
    _row_loop(tm, R, resid)

    pg[...] = jnp.dot(rb[...], wpg[...], preferred_element_type=f32)

    def ple(r):
        rows = pl.ds(r, R)
        x2 = x1b[rows, :] + _sigmoid(pg[rows, :]) * pe[rows, :]
        if final:
            ms = jnp.mean(x2 * x2, axis=-1, keepdims=True)
            x2 = x2 * lax.rsqrt(ms + NORM_EPS) * fg_ref[...]
        o_ref[0, rows, :] = x2
    _row_loop(tm, R, ple)


def _resident(shape):
    nd = len(shape)
    return pl.BlockSpec(shape, lambda b, t: (0,) * nd, pipeline_mode=pl.Buffered(1))


def _layer_call(layer, final, x, p, small, weights):
    batch, seq, d = x.shape
    tm = TM
    assert seq % tm == 0 and tm % A_BLOCK == 0 and d == D_MODEL
    assert len(weights) == N_WEIGHTS
    f32 = jnp.float32
    bf16 = jnp.bfloat16
    in_specs = ([pl.BlockSpec((1, tm, d), lambda b, t: (b, t, 0)),
                 pl.BlockSpec((1, 1, tm, PLE_DIM), lambda b, t: (layer, b, t, 0))]
                + [_resident(a.shape) for a in small]
                + [pl.BlockSpec(memory_space=pl.ANY) for _ in weights])
    act_f32 = pltpu.VMEM((tm, d), f32)
    act_bf16 = pltpu.VMEM((tm, d), bf16)
    kern = functools.partial(_layer_kernel, tm=tm, final=final)
    return pl.pallas_call(
        kern,
        grid=(batch, seq // tm),
        in_specs=in_specs,
        out_specs=pl.BlockSpec((1, tm, d), lambda b, t: (b, t, 0)),
        out_shape=jax.ShapeDtypeStruct(x.shape, f32),
        scratch_shapes=(
            [pltpu.VMEM(w.shape, bf16) for w in weights]
            + [pltpu.SemaphoreType.DMA((N_WEIGHTS,))]
            + [act_bf16,
               pltpu.VMEM((N_SEG, tm, SEG), f32),
               act_f32, act_f32, act_f32, act_f32, act_f32, act_f32,
               act_bf16, act_bf16, act_bf16, act_bf16, act_bf16,
               pltpu.VMEM((B_GROUPS, HALO + tm, LANES), f32),
               pltpu.VMEM((A_GROUPS, A_BLOCK, A_BLOCK), bf16)]),
        compiler_params=pltpu.CompilerParams(
            dimension_semantics=("arbitrary", "arbitrary"),
            vmem_limit_bytes=VMEM_LIMIT_BYTES),
        name=f"trunk_layer{layer}",
    )(x, p, *small, *weights)


def kernel(x, p, norm_g, w_in, b_in, a_ln_g, a_ln_b, a_ws, a_bs, b_conv_w, b_conv_b, b_gn_g,
           b_gn_b, w_pa, w_pb, w_out, ple_norm_g, w_ple_gate, w_ple, final_g):
    depth = w_in.shape[0]
    bf16 = jnp.bfloat16
    row = lambda a: a.reshape(1, -1)
    for i in range(depth):
        bs_full = jnp.repeat(a_bs[i].T, E_A // A_GROUPS, axis=1)
        small = (row(norm_g[i]), row(b_in[i]), row(a_ln_g[i]), row(a_ln_b[i]), a_ws[i], bs_full,
                 b_conv_w[i], row(b_conv_b[i]), row(b_gn_g[i]), row(b_gn_b[i]),
                 row(ple_norm_g[i]), row(final_g))
        weights = tuple(w[i].astype(bf16)
                        for w in (w_in, w_pa, w_pb, w_out, w_ple_gate, w_ple))
        x = _layer_call(i, i == depth - 1, x, p, small, weights)
    return x
```
